```python
import jax, jax.numpy as jnp
from jax import lax
import numpy as np

D_MODEL = 4096
BATCH = 1
SEQ = 8192
DEPTH = 4

CHUNK = 64
N_MIXERS = 2
CONV_KERNEL = 31
SB_HEAD_DIM = 128
SB_HEADS = D_MODEL // SB_HEAD_DIM
Q_BLOCK = 128
D_FF = ((8 * D_MODEL // 3 + 255) // 256) * 256
FFN_CONV = 3
EPS = 1e-6
N_CONV_LAYERS = (DEPTH + 1) // 2
N_SB_LAYERS = DEPTH // 2

kernel_name = "hybrid_conformer_stickbreaking_convffn"


def rmsnorm(x, g):
    xf = x.astype(jnp.float32)
    y = xf * lax.rsqrt(jnp.mean(xf * xf, axis=-1, keepdims=True) + EPS)
    return (y * g.astype(jnp.float32)).astype(x.dtype)


def layernorm(x, g, b):
    xf = x.astype(jnp.float32)
    mu = jnp.mean(xf, axis=-1, keepdims=True)
    xc = xf - mu
    var = jnp.mean(xc * xc, axis=-1, keepdims=True)
    y = xc * lax.rsqrt(var + EPS) * g.astype(jnp.float32) + b.astype(jnp.float32)
    return y.astype(x.dtype)


def causal_dwconv(x, w, b):
    width = w.shape[0]
    y = lax.conv_general_dilated(
        x, w[:, None, :].astype(x.dtype), window_strides=(1,), padding=[(width - 1, 0)],
        dimension_numbers=("NWC", "WIO", "NWC"), feature_group_count=x.shape[-1])
    return y + b


def conformer_conv(h, w_pw1, b_pw1, w_dw, b_dw, ln_g, ln_b, w_pw2, b_pw2):
    u = h @ w_pw1 + b_pw1
    a, gate = jnp.split(u, 2, axis=-1)
    u = a * jax.nn.sigmoid(gate)
    u = causal_dwconv(u, w_dw, b_dw)
    u = jax.nn.silu(layernorm(u, ln_g, ln_b))
    return u @ w_pw2 + b_pw2


def stick_breaking_attention(h, w_qkv, w_o):
    bsz, seq, _ = h.shape
    qkv = (h @ w_qkv).reshape(bsz, seq, 3, SB_HEADS, SB_HEAD_DIM)
    q = jnp.transpose(qkv[:, :, 0], (0, 2, 1, 3)).astype(jnp.float32)
    k = jnp.transpose(qkv[:, :, 1], (0, 2, 1, 3)).astype(jnp.float32)
    v = jnp.transpose(qkv[:, :, 2], (0, 2, 1, 3)).astype(jnp.float32)
    scale = SB_HEAD_DIM ** -0.5
    outs = []
    for blk in range(seq // Q_BLOCK):
        q0 = blk * Q_BLOCK
        k_end = q0 + Q_BLOCK
        qb = q[:, :, q0:k_end]
        kb = k[:, :, :k_end]
        vb = v[:, :, :k_end]
        z = jnp.einsum("bhqd,bhkd->bhqk", qb, kb) * scale
        t_idx = q0 + jnp.arange(Q_BLOCK)[:, None]
        s_idx = jnp.arange(k_end)[None, :]
        mask = s_idx < t_idx
        log_1m_beta = jnp.where(mask, jax.nn.log_sigmoid(-z), 0.0)
        later = lax.cumsum(log_1m_beta, axis=3, reverse=True) - log_1m_beta
        log_a = jax.nn.log_sigmoid(z) + later
        att = jnp.where(mask, jnp.exp(log_a), 0.0)
        outs.append(jnp.einsum("bhqk,bhkd->bhqd", att, vb))
    o = jnp.concatenate(outs, axis=2)
    o = jnp.transpose(o, (0, 2, 1, 3)).reshape(bsz, seq, SB_HEADS * SB_HEAD_DIM).astype(h.dtype)
    return o @ w_o


def conv_ffn(h, w_gate, w_up, w_dw, b_dw, w_down):
    g = causal_dwconv(h @ w_gate, w_dw, b_dw)
    u = h @ w_up
    return (jax.nn.silu(g) * u) @ w_down


def setup_inputs(seed: int = 0) -> dict:
    key = jax.random.key(seed)
    ks = jax.random.split(key, 24)
    f32 = jnp.float32
    D, F = D_MODEL, D_FF
    Lc, Ls, L = N_CONV_LAYERS, N_SB_LAYERS, DEPTH

    def nrm(k, shape, scale):
        return jax.random.normal(k, shape, f32) * scale

    return {
        "x": nrm(ks[0], (BATCH, SEQ, D), 1.0),
        "norm_mix": 1.0 + nrm(ks[1], (L, D), 0.02),
        "norm_ffn": 1.0 + nrm(ks[2], (L, D), 0.02),
        "final_norm": 1.0 + nrm(ks[3], (D,), 0.02),
        "cv_w_pw1": nrm(ks[4], (Lc, D, 2 * D), D ** -0.5),
        "cv_b_pw1": nrm(ks[5], (Lc, 2 * D), 0.02),
        "cv_w_dw": nrm(ks[6], (Lc, CONV_KERNEL, D), CONV_KERNEL ** -0.5),
        "cv_b_dw": nrm(ks[7], (Lc, D), 0.02),
        "cv_ln_g": 1.0 + nrm(ks[8], (Lc, D), 0.02),
        "cv_ln_b": nrm(ks[9], (Lc, D), 0.02),
        "cv_w_pw2": nrm(ks[10], (Lc, D, D), D ** -0.5),
        "cv_b_pw2": nrm(ks[11], (Lc, D), 0.02),
        "sb_w_qkv": nrm(ks[12], (Ls, D, 3 * D), D ** -0.5),
        "sb_w_o": nrm(ks[13], (Ls, D, D), D ** -0.5),
        "ff_w_gate": nrm(ks[14], (L, D, F), D ** -0.5),
        "ff_w_up": nrm(ks[15], (L, D, F), D ** -0.5),
        "ff_w_dw": nrm(ks[16], (L, FFN_CONV, F), FFN_CONV ** -0.5),
        "ff_b_dw": nrm(ks[17], (L, F), 0.02),
        "ff_w_down": nrm(ks[18], (L, F, D), F ** -0.5),
    }


def reference(x, norm_mix, norm_ffn, final_norm,
              cv_w_pw1, cv_b_pw1, cv_w_dw, cv_b_dw, cv_ln_g, cv_ln_b, cv_w_pw2, cv_b_pw2,
              sb_w_qkv, sb_w_o,
              ff_w_gate, ff_w_up, ff_w_dw, ff_b_dw, ff_w_down):
    for i in range(DEPTH):
        h = rmsnorm(x, norm_mix[i])
        j = i // N_MIXERS
        if i % N_MIXERS == 0:
            mix = conformer_conv(h, cv_w_pw1[j], cv_b_pw1[j], cv_w_dw[j], cv_b_dw[j],
                                 cv_ln_g[j], cv_ln_b[j], cv_w_pw2[j], cv_b_pw2[j])
        else:
            mix = stick_breaking_attention(h, sb_w_qkv[j], sb_w_o[j])
        x = x + mix
        h = rmsnorm(x, norm_ffn[i])
        x = x + conv_ffn(h, ff_w_gate[i], ff_w_up[i], ff_w_dw[i], ff_b_dw[i], ff_w_down[i])
    return rmsnorm(x, final_norm)
```

```python
import functools

import jax
import jax.numpy as jnp
from jax import lax
from jax.experimental import pallas as pl
from jax.experimental.pallas import tpu as pltpu

EPS = 1e-6
HEAD_DIM = 128
V7X_VMEM_LIMIT_BYTES = 60000 * 1024
F32 = jnp.float32
BF16 = jnp.bfloat16


def _params(semantics, vmem_bytes):
    return pltpu.CompilerParams(
        dimension_semantics=semantics,
        vmem_limit_bytes=min(int(vmem_bytes), V7X_VMEM_LIMIT_BYTES))


def _nbytes(shape, dtype):
    n = jnp.dtype(dtype).itemsize
    for s in shape:
        n *= s
    return n


def _rmsnorm_kernel(x_ref, g_ref, o_ref):
    x = x_ref[...]
    ms = jnp.mean(x * x, axis=-1, keepdims=True)
    o_ref[...] = ((x * lax.rsqrt(ms + EPS)) * g_ref[...]).astype(o_ref.dtype)


def rmsnorm(x, g, out_dtype, *, rows=256):
    s, d = x.shape
    rows = min(rows, s)
    vmem = 2 * rows * d * (4 + jnp.dtype(out_dtype).itemsize) + 4 * rows * d * 4
    return pl.pallas_call(
        _rmsnorm_kernel,
        grid=(s // rows,),
        in_specs=[pl.BlockSpec((rows, d), lambda i: (i, 0)),
                  pl.BlockSpec((1, d), lambda i: (0, 0))],
        out_specs=pl.BlockSpec((rows, d), lambda i: (i, 0)),
        out_shape=jax.ShapeDtypeStruct((s, d), out_dtype),
        compiler_params=_params(("parallel",), vmem),
        name="rmsnorm",
    )(x, g.reshape(1, d))


def _mm_plain_kernel(x_ref, w_ref, o_ref):
    o_ref[...] = jnp.dot(x_ref[...], w_ref[...],
                         preferred_element_type=F32).astype(o_ref.dtype)


def _mm_bias_res_kernel(x_ref, w_ref, b_ref, r_ref, o_ref):
    y = jnp.dot(x_ref[...], w_ref[...], preferred_element_type=F32)
    o_ref[...] = r_ref[...] + (y + b_ref[...])


def _mm_res_kernel(x_ref, w_ref, r_ref, o_ref):
    y = jnp.dot(x_ref[...], w_ref[...], preferred_element_type=F32)
    o_ref[...] = r_ref[...] + y


def _mm_glu_kernel(x_ref, wa_ref, wg_ref, ba_ref, bg_ref, o_ref):
    x = x_ref[...]
    a = jnp.dot(x, wa_ref[...], preferred_element_type=F32) + ba_ref[...]
    g = jnp.dot(x, wg_ref[...], preferred_element_type=F32) + bg_ref[...]
    o_ref[...] = a * jax.nn.sigmoid(g)


def matmul(x, w, *, out_dtype, bias=None, residual=None, tm=1024, tn=512):
    m, k = x.shape
    n = w.shape[1]
    tm, tn = min(tm, m), min(tn, n)
    in_specs = [pl.BlockSpec((tm, k), lambda i, j: (i, 0)),
                pl.BlockSpec((k, tn), lambda i, j: (0, j))]
    args = [x, w]
    if residual is None:
        assert bias is None
        body = _mm_plain_kernel
    else:
        if bias is not None:
            in_specs.append(pl.BlockSpec((1, tn), lambda i, j: (0, j)))
            args.append(bias.reshape(1, n))
            body = _mm_bias_res_kernel
        else:
            body = _mm_res_kernel
        in_specs.append(pl.BlockSpec((tm, tn), lambda i, j: (i, j)))
        args.append(residual)
    vmem = (2 * _nbytes((tm, k), BF16) + 2 * _nbytes((k, tn), BF16)
            + 2 * _nbytes((tm, tn), out_dtype) + 6 * _nbytes((tm, tn), F32))
    return pl.pallas_call(
        body,
        grid=(m // tm, n // tn),
        in_specs=in_specs,
        out_specs=pl.BlockSpec((tm, tn), lambda i, j: (i, j)),
        out_shape=jax.ShapeDtypeStruct((m, n), out_dtype),
        compiler_params=_params(("parallel", "parallel"), vmem),
        name="matmul",
    )(*args)


def matmul_glu(x, w, b, *, tm=1024, tn=512):
    m, k = x.shape
    n = w.shape[1] // 2
    tm, tn = min(tm, m), min(tn, n)
    nb = n // tn
    b2 = b.reshape(1, 2 * n)
    vmem = (2 * _nbytes((tm, k), BF16) + 4 * _nbytes((k, tn), BF16)
            + 10 * _nbytes((tm, tn), F32))
    return pl.pallas_call(
        _mm_glu_kernel,
        grid=(m // tm, nb),
        in_specs=[pl.BlockSpec((tm, k), lambda i, j: (i, 0)),
                  pl.BlockSpec((k, tn), lambda i, j: (0, j)),
                  pl.BlockSpec((k, tn), lambda i, j: (0, j + nb)),
                  pl.BlockSpec((1, tn), lambda i, j: (0, j)),
                  pl.BlockSpec((1, tn), lambda i, j: (0, j + nb))],
        out_specs=pl.BlockSpec((tm, tn), lambda i, j: (i, j)),
        out_shape=jax.ShapeDtypeStruct((m, n), F32),
        compiler_params=_params(("parallel", "parallel"), vmem),
        name="matmul_glu",
    )(x, w, w, b2, b2)


CONV_HALO = 32
CONV_ROWS = 64
LANES = 128


def _conv_ln_kernel(u_ref, halo_ref, w_ref, b_ref, g_ref, beta_ref, o_ref, buf_ref, y_ref,
                    *, taps):
    i = pl.program_id(0)
    tt, d = u_ref.shape
    buf_ref[0:CONV_HALO, :] = jnp.where(i > 0, halo_ref[...], 0.0)
    buf_ref[CONV_HALO:, :] = u_ref[...]
    first = CONV_HALO - (taps - 1)

    def col_body(c, _):
        col = pl.multiple_of(c * LANES, LANES)
        w = w_ref[:, pl.ds(col, LANES)]
        bias = b_ref[:, pl.ds(col, LANES)]
        for r0 in range(0, tt, CONV_ROWS):
            acc = jnp.broadcast_to(bias, (CONV_ROWS, LANES))
            for k in range(taps):
                acc = acc + w[k:k + 1, :] * buf_ref[pl.ds(first + r0 + k, CONV_ROWS),
                                                   pl.ds(col, LANES)]
            y_ref[pl.ds(r0, CONV_ROWS), pl.ds(col, LANES)] = acc
        return 0

    lax.fori_loop(0, d // LANES, col_body, 0)

    y = y_ref[...]
    mu = jnp.mean(y, axis=-1, keepdims=True)
    yc = y - mu
    var = jnp.mean(yc * yc, axis=-1, keepdims=True)
    z = yc * lax.rsqrt(var + EPS) * g_ref[...] + beta_ref[...]
    o_ref[...] = (z * jax.nn.sigmoid(z)).astype(o_ref.dtype)


def conv_ln_swish(u, w, b, ln_g, ln_b, *, rows=128):
    s, d = u.shape
    taps = w.shape[0]
    assert taps - 1 <= CONV_HALO
    rows = min(rows, s)
    assert rows % CONV_HALO == 0 and rows % CONV_ROWS == 0
    per = rows // CONV_HALO
    vmem = (2 * _nbytes((rows, d), F32) + 2 * _nbytes((CONV_HALO, d), F32)
            + 2 * _nbytes((rows, d), BF16) + _nbytes((rows + CONV_HALO, d), F32)
            + 5 * _nbytes((rows, d), F32))
    row = lambda v: v.reshape(1, d)
    return pl.pallas_call(
        functools.partial(_conv_ln_kernel, taps=taps),
        grid=(s // rows,),
        in_specs=[pl.BlockSpec((rows, d), lambda i: (i, 0)),
                  pl.BlockSpec((CONV_HALO, d), lambda i: (jnp.maximum(i * per - 1, 0), 0)),
                  pl.BlockSpec((taps, d), lambda i: (0, 0)),
                  pl.BlockSpec((1, d), lambda i: (0, 0)),
                  pl.BlockSpec((1, d), lambda i: (0, 0)),
                  pl.BlockSpec((1, d), lambda i: (0, 0))],
        out_specs=pl.BlockSpec((rows, d), lambda i: (i, 0)),
        out_shape=jax.ShapeDtypeStruct((s, d), BF16),
        scratch_shapes=[pltpu.VMEM((rows + CONV_HALO, d), F32),
                        pltpu.VMEM((rows, d), F32)],
        compiler_params=_params(("parallel",), vmem),
        name="conv_ln_swish",
    )(u, u, w, row(b), row(ln_g), row(ln_b))


def _softplus(z):
    return jnp.maximum(z, 0.0) + jnp.log1p(jnp.exp(-jnp.abs(z)))


def _sb_attn_kernel(q_ref, k_ref, v_ref, tri_ref, o_ref, *, blk, scale):
    i = pl.program_id(1)
    q = q_ref[...]
    tri = tri_ref[...]
    nt = (((1,), (1,)), ((), ()))

    def step(kb, acc, carry, masked):
        start = pl.multiple_of(kb * blk, blk)
        k = k_ref[pl.ds(start, blk), :]
        v = v_ref[pl.ds(start, blk), :]
        z = lax.dot_general(q, k, nt, preferred_element_type=F32) * scale
        sp = _softplus(z)
        if masked:
            t_idx = lax.broadcasted_iota(jnp.int32, (blk, blk), 0)
            s_idx = lax.broadcasted_iota(jnp.int32, (blk, blk), 1)
            mask = s_idx < t_idx
            sp = jnp.where(mask, sp, 0.0)
        hi = sp.astype(BF16)
        lo = (sp - hi.astype(F32)).astype(BF16)
        c = (jnp.dot(hi, tri, preferred_element_type=F32)
             + jnp.dot(lo, tri, preferred_element_type=F32))
        att = jnp.exp(z + (c + carry))
        if masked:
            att = jnp.where(mask, att, 0.0)
        acc = acc + jnp.dot(att.astype(BF16), v, preferred_element_type=F32)
        return acc, carry + c[:, 0:1]

    acc0 = jnp.zeros((blk, HEAD_DIM), F32)
    carry0 = jnp.zeros((blk, 1), F32)
    acc, carry = step(i, acc0, carry0, True)

    def body(n, state):
        return step(i - 1 - n, state[0], state[1], False)

    acc, _ = lax.fori_loop(0, i, body, (acc, carry))
    o_ref[...] = acc.astype(o_ref.dtype)


def stick_breaking_attention(qkv, heads, *, blk=256):
    s = qkv.shape[0]
    blk = min(blk, s)
    d = heads * HEAD_DIM
    j = lax.broadcasted_iota(jnp.int32, (blk, blk), 0)
    c = lax.broadcasted_iota(jnp.int32, (blk, blk), 1)
    tri = jnp.where(j >= c, -1.0, 0.0).astype(BF16)
    vmem = (4 * _nbytes((s, HEAD_DIM), BF16) + 4 * _nbytes((blk, HEAD_DIM), BF16)
            + 2 * _nbytes((blk, blk), BF16) + 12 * _nbytes((blk, blk), F32))
    return pl.pallas_call(
        functools.partial(_sb_attn_kernel, blk=blk, scale=HEAD_DIM ** -0.5),
        grid=(heads, s // blk),
        in_specs=[pl.BlockSpec((blk, HEAD_DIM), lambda h, i: (i, h)),
                  pl.BlockSpec((s, HEAD_DIM), lambda h, i: (0, heads + h)),
                  pl.BlockSpec((s, HEAD_DIM), lambda h, i: (0, 2 * heads + h)),
                  pl.BlockSpec((blk, blk), lambda h, i: (0, 0))],
        out_specs=pl.BlockSpec((blk, HEAD_DIM), lambda h, i: (i, h)),
        out_shape=jax.ShapeDtypeStruct((s, d), BF16),
        compiler_params=_params(("parallel", "parallel"), vmem),
        name="sb_attention",
    )(qkv, qkv, qkv, tri)


FFN_HALO = 8


def _ffn_kernel(x_ref, wg_ref, wu_ref, cw_ref, cb_ref, wd_ref, r_ref, o_ref, gbuf_ref, carry_ref):
    i = pl.program_id(0)
    f = pl.program_id(1)
    tm = x_ref.shape[0]
    x = x_ref[...]
    g = jnp.dot(x, wg_ref[...], preferred_element_type=F32)
    u = jnp.dot(x, wu_ref[...], preferred_element_type=F32)

    gbuf_ref[0:FFN_HALO, :] = jnp.where(i > 0, carry_ref[f], 0.0)
    gbuf_ref[FFN_HALO:, :] = g
    carry_ref[f] = g[tm - FFN_HALO:, :]
    cw = cw_ref[...]
    conv = (cw[0:1, :] * gbuf_ref[FFN_HALO - 2:FFN_HALO - 2 + tm, :]
            + cw[1:2, :] * gbuf_ref[FFN_HALO - 1:FFN_HALO - 1 + tm, :]
            + cw[2:3, :] * g) + cb_ref[...]
    act = ((conv * jax.nn.sigmoid(conv)) * u).astype(BF16)
    y = jnp.dot(act, wd_ref[...], preferred_element_type=F32)

    @pl.when(f == 0)
    def _():
        o_ref[...] = r_ref[...] + y

    @pl.when(f > 0)
    def _():
        o_ref[...] += y


def conv_ffn(h, resid, w_gate, w_up, cw, cb, w_down, *, tm=512, tf=256):
    s, d = h.shape
    fdim = w_gate.shape[1]
    tm = min(tm, s)
    assert fdim % tf == 0 and s % tm == 0
    nf = fdim // tf
    once = pl.Buffered(1)
    vmem = (_nbytes((tm, d), BF16) + _nbytes((tm, d), F32) + 2 * _nbytes((tm, d), F32)
            + 6 * _nbytes((d, tf), BF16) + _nbytes((nf, FFN_HALO, tf), F32)
            + 8 * _nbytes((tm, tf), F32) + _nbytes((tm, d), F32))
    return pl.pallas_call(
        _ffn_kernel,
        grid=(s // tm, nf),
        in_specs=[pl.BlockSpec((tm, d), lambda i, f: (i, 0), pipeline_mode=once),
                  pl.BlockSpec((d, tf), lambda i, f: (0, f)),
                  pl.BlockSpec((d, tf), lambda i, f: (0, f)),
                  pl.BlockSpec((cw.shape[0], tf), lambda i, f: (0, f)),
                  pl.BlockSpec((1, tf), lambda i, f: (0, f)),
                  pl.BlockSpec((tf, d), lambda i, f: (f, 0)),
                  pl.BlockSpec((tm, d), lambda i, f: (i, 0), pipeline_mode=once)],
        out_specs=pl.BlockSpec((tm, d), lambda i, f: (i, 0)),
        out_shape=jax.ShapeDtypeStruct((s, d), F32),
        scratch_shapes=[pltpu.VMEM((tm + FFN_HALO, tf), F32),
                        pltpu.VMEM((nf, FFN_HALO, tf), F32)],
        compiler_params=_params(("arbitrary", "arbitrary"), vmem),
        name="conv_ffn",
    )(h, w_gate, w_up, cw, cb.reshape(1, fdim), w_down, resid)


def kernel(x, norm_mix, norm_ffn, final_norm, cv_w_pw1, cv_b_pw1, cv_w_dw, cv_b_dw, cv_ln_g, cv_ln_b, cv_w_pw2, cv_b_pw2, sb_w_qkv, sb_w_o, ff_w_gate, ff_w_up, ff_w_dw, ff_b_dw, ff_w_down):
    bsz, seq, d = x.shape
    depth = norm_mix.shape[0]
    heads = d // HEAD_DIM
    outs = []
    for b in range(bsz):
        xb = x[b]
        for i in range(depth):
            j = i // 2
            h = rmsnorm(xb, norm_mix[i], BF16)
            if i % 2 == 0:
                u = matmul_glu(h, cv_w_pw1[j].astype(BF16), cv_b_pw1[j])
                u = conv_ln_swish(u, cv_w_dw[j], cv_b_dw[j], cv_ln_g[j], cv_ln_b[j])
                xb = matmul(u, cv_w_pw2[j].astype(BF16), out_dtype=F32,
                            bias=cv_b_pw2[j], residual=xb)
            else:
                qkv = matmul(h, sb_w_qkv[j].astype(BF16), out_dtype=BF16)
                o = stick_breaking_attention(qkv, heads)
                xb = matmul(o, sb_w_o[j].astype(BF16), out_dtype=F32, residual=xb)
            h = rmsnorm(xb, norm_ffn[i], BF16)
            xb = conv_ffn(h, xb, ff_w_gate[i].astype(BF16), ff_w_up[i].astype(BF16),
                          ff_w_dw[i], ff_b_dw[i], ff_w_down[i].astype(BF16))
        outs.append(rmsnorm(xb, final_norm, x.dtype))
    if bsz == 1:
        return outs[0].reshape(1, seq, d)
    return jnp.stack(outs, axis=0)
```

```python
import functools

import jax
import jax.numpy as jnp
from jax import lax
from jax.experimental import pallas as pl
from jax.experimental.pallas import tpu as pltpu

EPS = 1e-6
HEAD_DIM = 128
V7X_VMEM_LIMIT_BYTES = 60000 * 1024
F32 = jnp.float32
BF16 = jnp.bfloat16


def _params(semantics, vmem_bytes):
    return pltpu.CompilerParams(
        dimension_semantics=semantics,
        vmem_limit_bytes=min(int(vmem_bytes), V7X_VMEM_LIMIT_BYTES))


def _nbytes(shape, dtype):
    n = jnp.dtype(dtype).itemsize
    for s in shape:
        n *= s
    return n


def _rmsnorm_kernel(x_ref, g_ref, o_ref):
    x = x_ref[...]
    ms = jnp.mean(x * x, axis=-1, keepdims=True)
    o_ref[...] = ((x * lax.rsqrt(ms + EPS)) * g_ref[...]).astype(o_ref.dtype)


def rmsnorm(x, g, out_dtype, *, rows=256):
    s, d = x.shape
    rows = min(rows, s)
    vmem = 2 * rows * d * (4 + jnp.dtype(out_dtype).itemsize) + 4 * rows * d * 4
    return pl.pallas_call(
        _rmsnorm_kernel,
        grid=(s // rows,),
        in_specs=[pl.BlockSpec((rows, d), lambda i: (i, 0)),
                  pl.BlockSpec((1, d), lambda i: (0, 0))],
        out_specs=pl.BlockSpec((rows, d), lambda i: (i, 0)),
        out_shape=jax.ShapeDtypeStruct((s, d), out_dtype),
        compiler_params=_params(("parallel",), vmem),
        name="rmsnorm",
    )(x, g.reshape(1, d))


def _mm_plain_kernel(x_ref, w_ref, o_ref):
    o_ref[...] = jnp.dot(x_ref[...], w_ref[...],
                         preferred_element_type=F32).astype(o_ref.dtype)


def _mm_bias_res_kernel(x_ref, w_ref, b_ref, r_ref, o_ref):
    y = jnp.dot(x_ref[...], w_ref[...], preferred_element_type=F32)
    o_ref[...] = r_ref[...] + (y + b_ref[...])


def _mm_res_kernel(x_ref, w_ref, r_ref, o_ref):
    y = jnp.dot(x_ref[...], w_ref[...], preferred_element_type=F32)
    o_ref[...] = r_ref[...] + y


def _mm_glu_kernel(x_ref, wa_ref, wg_ref, ba_ref, bg_ref, o_ref):
    x = x_ref[...]
    a = jnp.dot(x, wa_ref[...], preferred_element_type=F32) + ba_ref[...]
    g = jnp.dot(x, wg_ref[...], preferred_element_type=F32) + bg_ref[...]
    o_ref[...] = a * jax.nn.sigmoid(g)


def _wspec(shape, idx, layer):
    if layer is None:
        return pl.BlockSpec(shape, idx)
    return pl.BlockSpec((None,) + shape, lambda *g: (layer,) + idx(*g))


def matmul(x, w, *, out_dtype, layer=None, bias=None, residual=None, tm=1024, tn=512):
    m, k = x.shape
    n = w.shape[-1]
    tm, tn = min(tm, m), min(tn, n)
    in_specs = [pl.BlockSpec((tm, k), lambda i, j: (i, 0)),
                _wspec((k, tn), lambda i, j: (0, j), layer)]
    args = [x, w]
    if residual is None:
        assert bias is None
        body = _mm_plain_kernel
    else:
        if bias is not None:
            in_specs.append(pl.BlockSpec((1, tn), lambda i, j: (0, j)))
            args.append(bias.reshape(1, n))
            body = _mm_bias_res_kernel
        else:
            body = _mm_res_kernel
        in_specs.append(pl.BlockSpec((tm, tn), lambda i, j: (i, j)))
        args.append(residual)
    vmem = (2 * _nbytes((tm, k), BF16) + 2 * _nbytes((k, tn), BF16)
            + 2 * _nbytes((tm, tn), out_dtype) + 6 * _nbytes((tm, tn), F32))
    return pl.pallas_call(
        body,
        grid=(m // tm, n // tn),
        in_specs=in_specs,
        out_specs=pl.BlockSpec((tm, tn), lambda i, j: (i, j)),
        out_shape=jax.ShapeDtypeStruct((m, n), out_dtype),
        compiler_params=_params(("parallel", "parallel"), vmem),
        name="matmul",
    )(*args)


def matmul_glu(x, w, b, *, layer=None, tm=1024, tn=512):
    m, k = x.shape
    n = w.shape[-1] // 2
    tm, tn = min(tm, m), min(tn, n)
    nb = n // tn
    b2 = b.reshape(1, 2 * n)
    vmem = (2 * _nbytes((tm, k), BF16) + 4 * _nbytes((k, tn), BF16)
            + 10 * _nbytes((tm, tn), F32))
    return pl.pallas_call(
        _mm_glu_kernel,
        grid=(m // tm, nb),
        in_specs=[pl.BlockSpec((tm, k), lambda i, j: (i, 0)),
                  _wspec((k, tn), lambda i, j: (0, j), layer),
                  _wspec((k, tn), lambda i, j: (0, j + nb), layer),
                  pl.BlockSpec((1, tn), lambda i, j: (0, j)),
                  pl.BlockSpec((1, tn), lambda i, j: (0, j + nb))],
        out_specs=pl.BlockSpec((tm, tn), lambda i, j: (i, j)),
        out_shape=jax.ShapeDtypeStruct((m, n), F32),
        compiler_params=_params(("parallel", "parallel"), vmem),
        name="matmul_glu",
    )(x, w, w, b2, b2)


CAST_BLOCK_BYTES = 8 * 1024 * 1024


def _cast_kernel(w_ref, o_ref):
    o_ref[...] = w_ref[...].astype(o_ref.dtype)


def cast_bf16(w):
    nl, k, n = w.shape
    rows = k
    while rows * n * 4 > CAST_BLOCK_BYTES and rows % 2 == 0 and (rows // 2) % 16 == 0:
        rows //= 2
    vmem = 2 * _nbytes((rows, n), F32) + 2 * _nbytes((rows, n), BF16) + _nbytes((rows, n), F32)
    return pl.pallas_call(
        _cast_kernel,
        grid=(nl, k // rows),
        in_specs=[pl.BlockSpec((None, rows, n), lambda l, r: (l, r, 0))],
        out_specs=pl.BlockSpec((None, rows, n), lambda l, r: (l, r, 0)),
        out_shape=jax.ShapeDtypeStruct(w.shape, BF16),
        compiler_params=_params(("parallel", "parallel"), vmem),
        name="cast_bf16",
    )(w)


CONV_HALO = 32
CONV_ROWS = 64
LANES = 128


def _conv_ln_kernel(u_ref, halo_ref, w_ref, b_ref, g_ref, beta_ref, o_ref, buf_ref, y_ref,
                    *, taps):
    i = pl.program_id(0)
    tt, d = u_ref.shape
    buf_ref[0:CONV_HALO, :] = jnp.where(i > 0, halo_ref[...], 0.0)
    buf_ref[CONV_HALO:, :] = u_ref[...]
    first = CONV_HALO - (taps - 1)

    def col_body(c, _):
        col = pl.multiple_of(c * LANES, LANES)
        w = w_ref[:, pl.ds(col, LANES)]
        bias = b_ref[:, pl.ds(col, LANES)]
        for r0 in range(0, tt, CONV_ROWS):
            acc = jnp.broadcast_to(bias, (CONV_ROWS, LANES))
            for k in range(taps):
                acc = acc + w[k:k + 1, :] * buf_ref[pl.ds(first + r0 + k, CONV_ROWS),
                                                   pl.ds(col, LANES)]
            y_ref[pl.ds(r0, CONV_ROWS), pl.ds(col, LANES)] = acc
        return 0

    lax.fori_loop(0, d // LANES, col_body, 0)

    y = y_ref[...]
    mu = jnp.mean(y, axis=-1, keepdims=True)
    yc = y - mu
    var = jnp.mean(yc * yc, axis=-1, keepdims=True)
    z = yc * lax.rsqrt(var + EPS) * g_ref[...] + beta_ref[...]
    o_ref[...] = (z * jax.nn.sigmoid(z)).astype(o_ref.dtype)


def conv_ln_swish(u, w, b, ln_g, ln_b, *, rows=128):
    s, d = u.shape
    taps = w.shape[0]
    assert taps - 1 <= CONV_HALO
    rows = min(rows, s)
    assert rows % CONV_HALO == 0 and rows % CONV_ROWS == 0
    per = rows // CONV_HALO
    vmem = (2 * _nbytes((rows, d), F32) + 2 * _nbytes((CONV_HALO, d), F32)
            + 2 * _nbytes((rows, d), BF16) + _nbytes((rows + CONV_HALO, d), F32)
            + 5 * _nbytes((rows, d), F32))
    row = lambda v: v.reshape(1, d)
    return pl.pallas_call(
        functools.partial(_conv_ln_kernel, taps=taps),
        grid=(s // rows,),
        in_specs=[pl.BlockSpec((rows, d), lambda i: (i, 0)),
                  pl.BlockSpec((CONV_HALO, d), lambda i: (jnp.maximum(i * per - 1, 0), 0)),
                  pl.BlockSpec((taps, d), lambda i: (0, 0)),
                  pl.BlockSpec((1, d), lambda i: (0, 0)),
                  pl.BlockSpec((1, d), lambda i: (0, 0)),
                  pl.BlockSpec((1, d), lambda i: (0, 0))],
        out_specs=pl.BlockSpec((rows, d), lambda i: (i, 0)),
        out_shape=jax.ShapeDtypeStruct((s, d), BF16),
        scratch_shapes=[pltpu.VMEM((rows + CONV_HALO, d), F32),
                        pltpu.VMEM((rows, d), F32)],
        compiler_params=_params(("parallel",), vmem),
        name="conv_ln_swish",
    )(u, u, w, row(b), row(ln_g), row(ln_b))


LOG2E = 1.4426950408889634
SIGN_BIT = 0x80000000
MASKED_LOGIT = -1e30


def _neg_abs(x):
    bits = lax.bitcast_convert_type(x, jnp.uint32) | jnp.uint32(SIGN_BIT)
    return lax.bitcast_convert_type(bits, F32)


def _sb_attn_kernel(q_ref, k_ref, v_ref, tri_ref, o_ref, z_buf, c_buf, acc_ref, carry_ref,
                    *, blk, heads_per_step, scale):
    i = pl.program_id(1)
    tri = tri_ref[...]
    nt = (((1,), (1,)), ((), ()))
    hp = heads_per_step
    lanes = [slice(h * HEAD_DIM, (h + 1) * HEAD_DIM) for h in range(hp)]
    qs = [q_ref[:, lanes[h]] for h in range(hp)]

    def key_start(n):
        return pl.multiple_of(jnp.maximum(i - n, 0) * blk, blk)

    def logits_stage(n):
        slot = n % 3
        start = key_start(n)
        for h in range(hp):
            k = k_ref[pl.ds(start, blk), lanes[h]]
            z_buf[slot, h] = (lax.dot_general(qs[h], k, nt, preferred_element_type=F32)
                              * (scale * LOG2E))

    def sums_stage(n, masked):
        if masked:
            t_idx = lax.broadcasted_iota(jnp.int32, (blk, blk), 0)
            s_idx = lax.broadcasted_iota(jnp.int32, (blk, blk), 1)
            mask = s_idx < t_idx
        for h in range(hp):
            z2 = z_buf[n % 3, h]
            sp2 = jnp.maximum(z2, 0.0) + jnp.log2(1.0 + jnp.exp2(_neg_abs(z2)))
            if masked:
                sp2 = jnp.where(mask, sp2, 0.0)
                z_buf[n % 3, h] = jnp.where(mask, z2, MASKED_LOGIT)
            hi = sp2.astype(BF16)
            lo = (sp2 - hi.astype(F32)).astype(BF16)
            c_buf[n % 2, h] = jnp.dot(jnp.concatenate([hi, lo], axis=1), tri,
                                      preferred_element_type=F32)

    def values_stage(n):
        start = key_start(n)
        for h in range(hp):
            c = c_buf[n % 2, h]
            carry = carry_ref[h]
            wide = jnp.concatenate([carry] * (blk // HEAD_DIM), axis=1)
            att = jnp.exp2(z_buf[n % 3, h] + (c + wide))
            v = v_ref[pl.ds(start, blk), lanes[h]]
            acc_ref[h] += jnp.dot(att.astype(BF16), v, preferred_element_type=F32)
            carry_ref[h] = carry + jnp.broadcast_to(c[:, 0:1], (blk, HEAD_DIM))

    acc_ref[...] = jnp.zeros_like(acc_ref)
    carry_ref[...] = jnp.zeros_like(carry_ref)
    logits_stage(0)
    sums_stage(0, True)
    logits_stage(1)

    def body(n, _):
        values_stage(n)
        sums_stage(n + 1, False)
        logits_stage(n + 2)
        return 0

    lax.fori_loop(0, i, body, 0)
    values_stage(i)
    for h in range(hp):
        o_ref[:, lanes[h]] = acc_ref[h].astype(o_ref.dtype)


def stick_breaking_attention(qkv, heads, *, blk=256, heads_per_step=2):
    s = qkv.shape[0]
    blk = min(blk, s)
    hp = heads_per_step
    assert heads % hp == 0 and s % blk == 0 and blk % HEAD_DIM == 0
    groups = heads // hp
    width = hp * HEAD_DIM
    j = lax.broadcasted_iota(jnp.int32, (2 * blk, blk), 0) % blk
    c = lax.broadcasted_iota(jnp.int32, (2 * blk, blk), 1)
    tri = jnp.where(j >= c, -1.0, 0.0).astype(BF16)
    vmem = (4 * _nbytes((s, width), BF16) + 4 * _nbytes((blk, width), BF16)
            + 2 * _nbytes((2 * blk, blk), BF16) + 12 * hp * _nbytes((blk, blk), F32))
    return pl.pallas_call(
        functools.partial(_sb_attn_kernel, blk=blk, heads_per_step=hp, scale=HEAD_DIM ** -0.5),
        grid=(groups, s // blk),
        in_specs=[pl.BlockSpec((blk, width), lambda g, i: (i, g)),
                  pl.BlockSpec((s, width), lambda g, i: (0, groups + g)),
                  pl.BlockSpec((s, width), lambda g, i: (0, 2 * groups + g)),
                  pl.BlockSpec((2 * blk, blk), lambda g, i: (0, 0))],
        out_specs=pl.BlockSpec((blk, width), lambda g, i: (i, g)),
        out_shape=jax.ShapeDtypeStruct((s, heads * HEAD_DIM), BF16),
        scratch_shapes=[pltpu.VMEM((3, hp, blk, blk), F32), pltpu.VMEM((2, hp, blk, blk), F32),
                        pltpu.VMEM((hp, blk, HEAD_DIM), F32), pltpu.VMEM((hp, blk, HEAD_DIM), F32)],
        compiler_params=_params(("parallel", "parallel"), vmem),
        name="sb_attention",
    )(qkv, qkv, qkv, tri)


FFN_HALO = 8
FFN_EPI_ROWS = 32
FFN_DOWN_COLS = 256


def _ffn_kernel(x_ref, wg_ref, wu_ref, cw_ref, cb_ref, wd_ref, r_ref, o_ref,
                g0_ref, g1_ref, u0_ref, u1_ref, a0_ref, a1_ref, carry_ref, *, nf):
    i = pl.program_id(0)
    f = pl.program_id(1)
    tm = x_ref.shape[0]
    g_refs, u_refs, a_refs = (g0_ref, g1_ref), (u0_ref, u1_ref), (a0_ref, a1_ref)

    @pl.when(f == 0)
    def _():
        o_ref[...] = r_ref[...]
        for ref in g_refs + u_refs + a_refs:
            ref[...] = jnp.zeros_like(ref)

    def step(p):
        g_new, u_new, a_old = g_refs[p], u_refs[p], a_refs[p]
        g_old, u_old, a_new = g_refs[1 - p], u_refs[1 - p], a_refs[1 - p]

        fe = jnp.clip(f - 1, 0, nf - 1)
        live = jnp.logical_and(f >= 1, f <= nf)
        old_tail = carry_ref[fe]
        g_old[0:FFN_HALO, :] = jnp.where(i > 0, old_tail, 0.0)
        carry_ref[fe] = jnp.where(live, g_old[tm:tm + FFN_HALO, :], old_tail)
        x = x_ref[...]
        g_new[FFN_HALO:, :] = jnp.dot(x, wg_ref[...], preferred_element_type=F32)
        u_new[...] = jnp.dot(x, wu_ref[...], preferred_element_type=F32)

        cw = cw_ref[...]
        cb = cb_ref[...]
        d = o_ref.shape[1]
        pieces = max(1, min(tm // FFN_EPI_ROWS, d // FFN_DOWN_COLS))
        rows, cols = tm // pieces, d // pieces
        a_prev = a_old[...]
        for c in range(pieces):
            o_ref[:, pl.ds(c * cols, cols)] += jnp.dot(
                a_prev, wd_ref[:, pl.ds(c * cols, cols)], preferred_element_type=F32)
            r = c * rows
            conv = (cw[0:1, :] * g_old[pl.ds(FFN_HALO - 2 + r, rows), :]
                    + cw[1:2, :] * g_old[pl.ds(FFN_HALO - 1 + r, rows), :]
                    + cw[2:3, :] * g_old[pl.ds(FFN_HALO + r, rows), :]) + cb
            a_new[pl.ds(r, rows), :] = (
                (conv * jax.nn.sigmoid(conv)) * u_old[pl.ds(r, rows), :]).astype(BF16)

    @pl.when(f % 2 == 0)
    def _():
        step(0)

    @pl.when(f % 2 == 1)
    def _():
        step(1)


def conv_ffn(h, resid, w_gate, w_up, cw, cb, w_down, *, layer=None, tm=512, tf=256):
    s, d = h.shape
    fdim = w_gate.shape[-1]
    tm = min(tm, s)
    assert fdim % tf == 0 and s % tm == 0
    nf = fdim // tf
    once = pl.Buffered(1)

    wspec = lambda shape, idx: _wspec(shape, idx, layer)
    col = lambda i, f: (0, jnp.minimum(f, nf - 1))
    ecol = lambda i, f: (0, jnp.clip(f - 1, 0, nf - 1))
    vmem = (_nbytes((tm, d), BF16) + _nbytes((tm, d), F32) + 2 * _nbytes((tm, d), F32)
            + 6 * _nbytes((d, tf), BF16) + _nbytes((nf, FFN_HALO, tf), F32)
            + 14 * _nbytes((tm, tf), F32) + _nbytes((tm, d), F32))
    return pl.pallas_call(
        functools.partial(_ffn_kernel, nf=nf),
        grid=(s // tm, nf + 2),
        in_specs=[pl.BlockSpec((tm, d), lambda i, f: (i, 0), pipeline_mode=once),
                  wspec((d, tf), col),
                  wspec((d, tf), col),
                  wspec((cw.shape[-2], tf), ecol),
                  wspec((1, tf), ecol),
                  wspec((tf, d), lambda i, f: (jnp.clip(f - 2, 0, nf - 1), 0)),
                  pl.BlockSpec((tm, d), lambda i, f: (i, 0), pipeline_mode=once)],
        out_specs=pl.BlockSpec((tm, d), lambda i, f: (i, 0)),
        out_shape=jax.ShapeDtypeStruct((s, d), F32),
        scratch_shapes=[pltpu.VMEM((tm + FFN_HALO, tf), F32), pltpu.VMEM((tm + FFN_HALO, tf), F32),
                        pltpu.VMEM((tm, tf), F32), pltpu.VMEM((tm, tf), F32),
                        pltpu.VMEM((tm, tf), BF16), pltpu.VMEM((tm, tf), BF16),
                        pltpu.VMEM((nf, FFN_HALO, tf), F32)],
        compiler_params=_params(("arbitrary", "arbitrary"), vmem),
        name="conv_ffn",
    )(h, w_gate, w_up, cw, cb, w_down, resid)


def kernel(x, norm_mix, norm_ffn, final_norm, cv_w_pw1, cv_b_pw1, cv_w_dw, cv_b_dw, cv_ln_g, cv_ln_b, cv_w_pw2, cv_b_pw2, sb_w_qkv, sb_w_o, ff_w_gate, ff_w_up, ff_w_dw, ff_b_dw, ff_w_down):
    bsz, seq, d = x.shape
    depth = norm_mix.shape[0]
    heads = d // HEAD_DIM
    fdim = ff_w_gate.shape[-1]
    w_pw1, w_pw2 = cast_bf16(cv_w_pw1), cast_bf16(cv_w_pw2)
    w_qkv, w_o = cast_bf16(sb_w_qkv), cast_bf16(sb_w_o)
    w_gate, w_up, w_down = cast_bf16(ff_w_gate), cast_bf16(ff_w_up), cast_bf16(ff_w_down)
    ff_b = ff_b_dw.reshape(depth, 1, fdim)
    outs = []
    for b in range(bsz):
        xb = x[b]
        for i in range(depth):
            j = i // 2
            h = rmsnorm(xb, norm_mix[i], BF16)
            if i % 2 == 0:
                u = matmul_glu(h, w_pw1, cv_b_pw1[j], layer=j)
                u = conv_ln_swish(u, cv_w_dw[j], cv_b_dw[j], cv_ln_g[j], cv_ln_b[j])
                xb = matmul(u, w_pw2, layer=j, out_dtype=F32, bias=cv_b_pw2[j], residual=xb)
            else:
                qkv = matmul(h, w_qkv, layer=j, out_dtype=BF16)
                o = stick_breaking_attention(qkv, heads)
                xb = matmul(o, w_o, layer=j, out_dtype=F32, residual=xb)
            h = rmsnorm(xb, norm_ffn[i], BF16)
            xb = conv_ffn(h, xb, w_gate, w_up, ff_w_dw, ff_b, w_down, layer=i)
        outs.append(rmsnorm(xb, final_norm, x.dtype))
    if bsz == 1:
        return outs[0].reshape(1, seq, d)
    return jnp.stack(outs, axis=0)
```

```python
import functools

import jax
import jax.numpy as jnp
from jax import lax
from jax.experimental import pallas as pl
from jax.experimental.pallas import tpu as pltpu

EPS = 1e-6
HEAD_DIM = 128
V7X_VMEM_LIMIT_BYTES = 60000 * 1024
F32 = jnp.float32
BF16 = jnp.bfloat16


def _params(semantics, vmem_bytes):
    return pltpu.CompilerParams(
        dimension_semantics=semantics,
        vmem_limit_bytes=min(int(vmem_bytes), V7X_VMEM_LIMIT_BYTES))


def _nbytes(shape, dtype):
    n = jnp.dtype(dtype).itemsize
    for s in shape:
        n *= s
    return n


def _rmsnorm_kernel(x_ref, g_ref, o_ref):
    x = x_ref[...]
    ms = jnp.mean(x * x, axis=-1, keepdims=True)
    o_ref[...] = ((x * lax.rsqrt(ms + EPS)) * g_ref[...]).astype(o_ref.dtype)


def rmsnorm(x, g, out_dtype, *, rows=256):
    s, d = x.shape
    rows = min(rows, s)
    vmem = 2 * rows * d * (4 + jnp.dtype(out_dtype).itemsize) + 4 * rows * d * 4
    return pl.pallas_call(
        _rmsnorm_kernel,
        grid=(s // rows,),
        in_specs=[pl.BlockSpec((rows, d), lambda i: (i, 0)),
                  pl.BlockSpec((1, d), lambda i: (0, 0))],
        out_specs=pl.BlockSpec((rows, d), lambda i: (i, 0)),
        out_shape=jax.ShapeDtypeStruct((s, d), out_dtype),
        compiler_params=_params(("parallel",), vmem),
        name="rmsnorm",
    )(x, g.reshape(1, d))


def _mm_plain_kernel(x_ref, w_ref, o_ref):
    o_ref[...] = jnp.dot(x_ref[...], w_ref[...],
                         preferred_element_type=F32).astype(o_ref.dtype)


def _mm_bias_res_kernel(x_ref, w_ref, b_ref, r_ref, o_ref):
    y = jnp.dot(x_ref[...], w_ref[...], preferred_element_type=F32)
    o_ref[...] = r_ref[...] + (y + b_ref[...])


def _mm_res_kernel(x_ref, w_ref, r_ref, o_ref):
    y = jnp.dot(x_ref[...], w_ref[...], preferred_element_type=F32)
    o_ref[...] = r_ref[...] + y


def _mm_glu_kernel(x_ref, wa_ref, wg_ref, ba_ref, bg_ref, o_ref):
    x = x_ref[...]
    a = jnp.dot(x, wa_ref[...], preferred_element_type=F32) + ba_ref[...]
    g = jnp.dot(x, wg_ref[...], preferred_element_type=F32) + bg_ref[...]
    o_ref[...] = a * jax.nn.sigmoid(g)


def _wspec(shape, idx, layer):
    if layer is None:
        return pl.BlockSpec(shape, idx)
    return pl.BlockSpec((None,) + shape, lambda *g: (layer,) + idx(*g))


def matmul(x, w, *, out_dtype, layer=None, bias=None, residual=None, tm=1024, tn=512):
    m, k = x.shape
    n = w.shape[-1]
    tm, tn = min(tm, m), min(tn, n)
    in_specs = [pl.BlockSpec((tm, k), lambda i, j: (i, 0)),
                _wspec((k, tn), lambda i, j: (0, j), layer)]
    args = [x, w]
    if residual is None:
        assert bias is None
        body = _mm_plain_kernel
    else:
        if bias is not None:
            in_specs.append(pl.BlockSpec((1, tn), lambda i, j: (0, j)))
            args.append(bias.reshape(1, n))
            body = _mm_bias_res_kernel
        else:
            body = _mm_res_kernel
        in_specs.append(pl.BlockSpec((tm, tn), lambda i, j: (i, j)))
        args.append(residual)
    vmem = (2 * _nbytes((tm, k), BF16) + 2 * _nbytes((k, tn), BF16)
            + 2 * _nbytes((tm, tn), out_dtype) + 6 * _nbytes((tm, tn), F32))
    return pl.pallas_call(
        body,
        grid=(m // tm, n // tn),
        in_specs=in_specs,
        out_specs=pl.BlockSpec((tm, tn), lambda i, j: (i, j)),
        out_shape=jax.ShapeDtypeStruct((m, n), out_dtype),
        compiler_params=_params(("parallel", "parallel"), vmem),
        name="matmul",
    )(*args)


def matmul_glu(x, w, b, *, layer=None, tm=1024, tn=512):
    m, k = x.shape
    n = w.shape[-1] // 2
    tm, tn = min(tm, m), min(tn, n)
    nb = n // tn
    b2 = b.reshape(1, 2 * n)
    vmem = (2 * _nbytes((tm, k), BF16) + 4 * _nbytes((k, tn), BF16)
            + 10 * _nbytes((tm, tn), F32))
    return pl.pallas_call(
        _mm_glu_kernel,
        grid=(m // tm, nb),
        in_specs=[pl.BlockSpec((tm, k), lambda i, j: (i, 0)),
                  _wspec((k, tn), lambda i, j: (0, j), layer),
                  _wspec((k, tn), lambda i, j: (0, j + nb), layer),
                  pl.BlockSpec((1, tn), lambda i, j: (0, j)),
                  pl.BlockSpec((1, tn), lambda i, j: (0, j + nb))],
        out_specs=pl.BlockSpec((tm, tn), lambda i, j: (i, j)),
        out_shape=jax.ShapeDtypeStruct((m, n), F32),
        compiler_params=_params(("parallel", "parallel"), vmem),
        name="matmul_glu",
    )(x, w, w, b2, b2)


CAST_BLOCK_BYTES = 8 * 1024 * 1024


def _cast_kernel(w_ref, o_ref):
    o_ref[...] = w_ref[...].astype(o_ref.dtype)


def cast_bf16(w):
    nl, k, n = w.shape
    rows = k
    while rows * n * 4 > CAST_BLOCK_BYTES and rows % 2 == 0 and (rows // 2) % 16 == 0:
        rows //= 2
    vmem = 2 * _nbytes((rows, n), F32) + 2 * _nbytes((rows, n), BF16) + _nbytes((rows, n), F32)
    return pl.pallas_call(
        _cast_kernel,
        grid=(nl, k // rows),
        in_specs=[pl.BlockSpec((None, rows, n), lambda l, r: (l, r, 0))],
        out_specs=pl.BlockSpec((None, rows, n), lambda l, r: (l, r, 0)),
        out_shape=jax.ShapeDtypeStruct(w.shape, BF16),
        compiler_params=_params(("parallel", "parallel"), vmem),
        name="cast_bf16",
    )(w)


CONV_HALO = 32
CONV_STRIDE = 4
CONV_SUB = 8 * CONV_STRIDE
LANES = 128


def _conv_ln_kernel(u_ref, halo_ref, w_ref, b_ref, g_ref, beta_ref, o_ref, buf_ref, y_ref,
                    *, taps):
    i = pl.program_id(0)
    tt, d = u_ref.shape
    slabs = d // LANES
    first = CONV_HALO - (taps - 1)

    def col_body(c, _):
        col = pl.multiple_of(c * LANES, LANES)
        buf_ref[c, 0:CONV_HALO, :] = jnp.where(i > 0, halo_ref[:, pl.ds(col, LANES)], 0.0)
        buf_ref[c, CONV_HALO:, :] = u_ref[:, pl.ds(col, LANES)]
        w = w_ref[:, pl.ds(col, LANES)]
        bias = jnp.broadcast_to(b_ref[:, pl.ds(col, LANES)], (8, LANES))
        wk = [jnp.broadcast_to(w[k:k + 1, :], (8, LANES)) for k in range(taps)]
        for r0 in range(0, tt, CONV_SUB):
            for j in range(CONV_STRIDE):
                acc = bias
                for k in range(taps):
                    acc = acc + wk[k] * buf_ref[c, pl.ds(first + r0 + j + k, 8, stride=CONV_STRIDE), :]
                y_ref[c, pl.ds(r0 + j, 8, stride=CONV_STRIDE), :] = acc
        return 0

    lax.fori_loop(0, slabs, col_body, 0)

    y = y_ref[...]
    mu = jnp.sum(jnp.sum(y, axis=0), axis=-1, keepdims=True) * (1.0 / d)
    yc = y - mu[None]
    var = jnp.sum(jnp.sum(yc * yc, axis=0), axis=-1, keepdims=True) * (1.0 / d)
    inv = lax.rsqrt(var + EPS)
    for c in range(slabs):
        cols = slice(c * LANES, (c + 1) * LANES)
        z = yc[c] * inv * g_ref[:, cols] + beta_ref[:, cols]
        o_ref[:, cols] = (z * jax.nn.sigmoid(z)).astype(o_ref.dtype)


def conv_ln_swish(u, w, b, ln_g, ln_b, *, rows=128):
    s, d = u.shape
    taps = w.shape[0]
    assert taps - 1 <= CONV_HALO and d % LANES == 0
    rows = min(rows, s)
    assert rows % CONV_HALO == 0 and rows % CONV_SUB == 0
    per = rows // CONV_HALO
    vmem = (2 * _nbytes((rows, d), F32) + 2 * _nbytes((CONV_HALO, d), F32)
            + 2 * _nbytes((rows, d), BF16) + _nbytes((rows + CONV_HALO, d), F32)
            + 6 * _nbytes((rows, d), F32))
    row = lambda v: v.reshape(1, d)
    return pl.pallas_call(
        functools.partial(_conv_ln_kernel, taps=taps),
        grid=(s // rows,),
        in_specs=[pl.BlockSpec((rows, d), lambda i: (i, 0)),
                  pl.BlockSpec((CONV_HALO, d), lambda i: (jnp.maximum(i * per - 1, 0), 0)),
                  pl.BlockSpec((taps, d), lambda i: (0, 0)),
                  pl.BlockSpec((1, d), lambda i: (0, 0)),
                  pl.BlockSpec((1, d), lambda i: (0, 0)),
                  pl.BlockSpec((1, d), lambda i: (0, 0))],
        out_specs=pl.BlockSpec((rows, d), lambda i: (i, 0)),
        out_shape=jax.ShapeDtypeStruct((s, d), BF16),
        scratch_shapes=[pltpu.VMEM((d // LANES, rows + CONV_HALO, LANES), F32),
                        pltpu.VMEM((d // LANES, rows, LANES), F32)],
        compiler_params=_params(("parallel",), vmem),
        name="conv_ln_swish",
    )(u, u, w, row(b), row(ln_g), row(ln_b))


LOG2E = 1.4426950408889634
SIGN_BIT = 0x80000000
MASKED_LOGIT = -1e30


def _neg_abs(x):
    bits = lax.bitcast_convert_type(x, jnp.uint32) | jnp.uint32(SIGN_BIT)
    return lax.bitcast_convert_type(bits, F32)


def _sb_attn_kernel(q_ref, k_ref, v_ref, tri_ref, o_ref, z_buf, c_buf, acc_ref, carry_ref,
                    *, blk, heads_per_step, scale):
    i = pl.program_id(1)
    tri = tri_ref[...]
    nt = (((1,), (1,)), ((), ()))
    hp = heads_per_step
    lanes = [slice(h * HEAD_DIM, (h + 1) * HEAD_DIM) for h in range(hp)]
    qs = [q_ref[:, lanes[h]] for h in range(hp)]

    def key_start(n):
        return pl.multiple_of(jnp.maximum(i - n, 0) * blk, blk)

    def logits_stage(n):
        slot = n % 3
        start = key_start(n)
        for h in range(hp):
            k = k_ref[pl.ds(start, blk), lanes[h]]
            z_buf[slot, h] = (lax.dot_general(qs[h], k, nt, preferred_element_type=F32)
                              * (scale * LOG2E))

    def sums_stage(n, masked):
        if masked:
            t_idx = lax.broadcasted_iota(jnp.int32, (blk, blk), 0)
            s_idx = lax.broadcasted_iota(jnp.int32, (blk, blk), 1)
            mask = s_idx < t_idx
        for h in range(hp):
            z2 = z_buf[n % 3, h]
            sp2 = jnp.maximum(z2, 0.0) + jnp.log2(1.0 + jnp.exp2(_neg_abs(z2)))
            if masked:
                sp2 = jnp.where(mask, sp2, 0.0)
                z_buf[n % 3, h] = jnp.where(mask, z2, MASKED_LOGIT)
            hi = sp2.astype(BF16)
            lo = (sp2 - hi.astype(F32)).astype(BF16)
            c_buf[n % 2, h] = jnp.dot(jnp.concatenate([hi, lo], axis=1), tri,
                                      preferred_element_type=F32)

    def values_stage(n):
        start = key_start(n)
        for h in range(hp):
            c = c_buf[n % 2, h]
            carry = carry_ref[h]
            wide = jnp.concatenate([carry] * (blk // HEAD_DIM), axis=1)
            att = jnp.exp2(z_buf[n % 3, h] + (c + wide))
            v = v_ref[pl.ds(start, blk), lanes[h]]
            acc_ref[h] += jnp.dot(att.astype(BF16), v, preferred_element_type=F32)
            carry_ref[h] = carry + jnp.broadcast_to(c[:, 0:1], (blk, HEAD_DIM))

    acc_ref[...] = jnp.zeros_like(acc_ref)
    carry_ref[...] = jnp.zeros_like(carry_ref)
    logits_stage(0)
    sums_stage(0, True)
    logits_stage(1)

    def body(n, _):
        values_stage(n)
        sums_stage(n + 1, False)
        logits_stage(n + 2)
        return 0

    lax.fori_loop(0, i, body, 0)
    values_stage(i)
    for h in range(hp):
        o_ref[:, lanes[h]] = acc_ref[h].astype(o_ref.dtype)


def stick_breaking_attention(qkv, heads, *, blk=256, heads_per_step=4):
    s = qkv.shape[0]
    blk = min(blk, s)
    hp = heads_per_step
    assert heads % hp == 0 and s % blk == 0 and blk % HEAD_DIM == 0
    groups = heads // hp
    width = hp * HEAD_DIM
    j = lax.broadcasted_iota(jnp.int32, (2 * blk, blk), 0) % blk
    c = lax.broadcasted_iota(jnp.int32, (2 * blk, blk), 1)
    tri = jnp.where(j >= c, -1.0, 0.0).astype(BF16)
    vmem = (4 * _nbytes((s, width), BF16) + 4 * _nbytes((blk, width), BF16)
            + 2 * _nbytes((2 * blk, blk), BF16) + 12 * hp * _nbytes((blk, blk), F32))
    return pl.pallas_call(
        functools.partial(_sb_attn_kernel, blk=blk, heads_per_step=hp, scale=HEAD_DIM ** -0.5),
        grid=(groups, s // blk),
        in_specs=[pl.BlockSpec((blk, width), lambda g, i: (i, g)),
                  pl.BlockSpec((s, width), lambda g, i: (0, groups + g)),
                  pl.BlockSpec((s, width), lambda g, i: (0, 2 * groups + g)),
                  pl.BlockSpec((2 * blk, blk), lambda g, i: (0, 0))],
        out_specs=pl.BlockSpec((blk, width), lambda g, i: (i, g)),
        out_shape=jax.ShapeDtypeStruct((s, heads * HEAD_DIM), BF16),
        scratch_shapes=[pltpu.VMEM((3, hp, blk, blk), F32), pltpu.VMEM((2, hp, blk, blk), F32),
                        pltpu.VMEM((hp, blk, HEAD_DIM), F32), pltpu.VMEM((hp, blk, HEAD_DIM), F32)],
        compiler_params=_params(("parallel", "parallel"), vmem),
        name="sb_attention",
    )(qkv, qkv, qkv, tri)


FFN_HALO = 8
FFN_EPI_ROWS = 32
FFN_DOWN_COLS = 256


def _ffn_kernel(x_ref, wg_ref, wu_ref, cw_ref, cb_ref, wd_ref, r_ref, o_ref,
                g0_ref, g1_ref, u0_ref, u1_ref, a0_ref, a1_ref, carry_ref, *, nf):
    i = pl.program_id(0)
    f = pl.program_id(1)
    tm = x_ref.shape[0]
    g_refs, u_refs, a_refs = (g0_ref, g1_ref), (u0_ref, u1_ref), (a0_ref, a1_ref)

    @pl.when(f == 0)
    def _():
        o_ref[...] = r_ref[...]
        for ref in g_refs + u_refs + a_refs:
            ref[...] = jnp.zeros_like(ref)

    def step(p):
        g_new, u_new, a_old = g_refs[p], u_refs[p], a_refs[p]
        g_old, u_old, a_new = g_refs[1 - p], u_refs[1 - p], a_refs[1 - p]

        fe = jnp.clip(f - 1, 0, nf - 1)
        live = jnp.logical_and(f >= 1, f <= nf)
        old_tail = carry_ref[fe]
        g_old[0:FFN_HALO, :] = jnp.where(i > 0, old_tail, 0.0)
        carry_ref[fe] = jnp.where(live, g_old[tm:tm + FFN_HALO, :], old_tail)
        x = x_ref[...]
        g_new[FFN_HALO:, :] = jnp.dot(x, wg_ref[...], preferred_element_type=F32)
        u_new[...] = jnp.dot(x, wu_ref[...], preferred_element_type=F32)

        cw = cw_ref[...]
        cb = cb_ref[...]
        d = o_ref.shape[1]
        pieces = max(1, min(tm // FFN_EPI_ROWS, d // FFN_DOWN_COLS))
        rows, cols = tm // pieces, d // pieces
        a_prev = a_old[...]
        for c in range(pieces):
            o_ref[:, pl.ds(c * cols, cols)] += jnp.dot(
                a_prev, wd_ref[:, pl.ds(c * cols, cols)], preferred_element_type=F32)
            r = c * rows
            conv = (cw[0:1, :] * g_old[pl.ds(FFN_HALO - 2 + r, rows), :]
                    + cw[1:2, :] * g_old[pl.ds(FFN_HALO - 1 + r, rows), :]
                    + cw[2:3, :] * g_old[pl.ds(FFN_HALO + r, rows), :]) + cb
            a_new[pl.ds(r, rows), :] = (
                (conv * jax.nn.sigmoid(conv)) * u_old[pl.ds(r, rows), :]).astype(BF16)

    @pl.when(f % 2 == 0)
    def _():
        step(0)

    @pl.when(f % 2 == 1)
    def _():
        step(1)


def conv_ffn(h, resid, w_gate, w_up, cw, cb, w_down, *, layer=None, tm=512, tf=256):
    s, d = h.shape
    fdim = w_gate.shape[-1]
    tm = min(tm, s)
    assert fdim % tf == 0 and s % tm == 0
    nf = fdim // tf
    once = pl.Buffered(1)

    wspec = lambda shape, idx: _wspec(shape, idx, layer)
    col = lambda i, f: (0, jnp.minimum(f, nf - 1))
    ecol = lambda i, f: (0, jnp.clip(f - 1, 0, nf - 1))
    vmem = (_nbytes((tm, d), BF16) + _nbytes((tm, d), F32) + 2 * _nbytes((tm, d), F32)
            + 6 * _nbytes((d, tf), BF16) + _nbytes((nf, FFN_HALO, tf), F32)
            + 14 * _nbytes((tm, tf), F32) + _nbytes((tm, d), F32))
    return pl.pallas_call(
        functools.partial(_ffn_kernel, nf=nf),
        grid=(s // tm, nf + 2),
        in_specs=[pl.BlockSpec((tm, d), lambda i, f: (i, 0), pipeline_mode=once),
                  wspec((d, tf), col),
                  wspec((d, tf), col),
                  wspec((cw.shape[-2], tf), ecol),
                  wspec((1, tf), ecol),
                  wspec((tf, d), lambda i, f: (jnp.clip(f - 2, 0, nf - 1), 0)),
                  pl.BlockSpec((tm, d), lambda i, f: (i, 0), pipeline_mode=once)],
        out_specs=pl.BlockSpec((tm, d), lambda i, f: (i, 0)),
        out_shape=jax.ShapeDtypeStruct((s, d), F32),
        scratch_shapes=[pltpu.VMEM((tm + FFN_HALO, tf), F32), pltpu.VMEM((tm + FFN_HALO, tf), F32),
                        pltpu.VMEM((tm, tf), F32), pltpu.VMEM((tm, tf), F32),
                        pltpu.VMEM((tm, tf), BF16), pltpu.VMEM((tm, tf), BF16),
                        pltpu.VMEM((nf, FFN_HALO, tf), F32)],
        compiler_params=_params(("arbitrary", "arbitrary"), vmem),
        name="conv_ffn",
    )(h, w_gate, w_up, cw, cb, w_down, resid)


def kernel(x, norm_mix, norm_ffn, final_norm, cv_w_pw1, cv_b_pw1, cv_w_dw, cv_b_dw, cv_ln_g, cv_ln_b, cv_w_pw2, cv_b_pw2, sb_w_qkv, sb_w_o, ff_w_gate, ff_w_up, ff_w_dw, ff_b_dw, ff_w_down):
    bsz, seq, d = x.shape
    depth = norm_mix.shape[0]
    heads = d // HEAD_DIM
    fdim = ff_w_gate.shape[-1]
    w_pw1, w_pw2 = cast_bf16(cv_w_pw1), cast_bf16(cv_w_pw2)
    w_qkv, w_o = cast_bf16(sb_w_qkv), cast_bf16(sb_w_o)
    w_gate, w_up, w_down = cast_bf16(ff_w_gate), cast_bf16(ff_w_up), cast_bf16(ff_w_down)
    ff_b = ff_b_dw.reshape(depth, 1, fdim)
    outs = []
    for b in range(bsz):
        xb = x[b]
        for i in range(depth):
            j = i // 2
            h = rmsnorm(xb, norm_mix[i], BF16)
            if i % 2 == 0:
                u = matmul_glu(h, w_pw1, cv_b_pw1[j], layer=j)
                u = conv_ln_swish(u, cv_w_dw[j], cv_b_dw[j], cv_ln_g[j], cv_ln_b[j])
                xb = matmul(u, w_pw2, layer=j, out_dtype=F32, bias=cv_b_pw2[j], residual=xb)
            else:
                qkv = matmul(h, w_qkv, layer=j, out_dtype=BF16)
                o = stick_breaking_attention(qkv, heads)
                xb = matmul(o, w_o, layer=j, out_dtype=F32, residual=xb)
            h = rmsnorm(xb, norm_ffn[i], BF16)
            xb = conv_ffn(h, xb, w_gate, w_up, ff_w_dw, ff_b, w_down, layer=i)
        outs.append(rmsnorm(xb, final_norm, x.dtype))
    if bsz == 1:
        return outs[0].reshape(1, seq, d)
    return jnp.stack(outs, axis=0)
```

```python
import functools

import jax
import jax.numpy as jnp
from jax import lax
from jax.experimental import pallas as pl
from jax.experimental.pallas import tpu as pltpu

EPS = 1e-6
HEAD_DIM = 128
V7X_VMEM_LIMIT_BYTES = 60000 * 1024
F32 = jnp.float32
BF16 = jnp.bfloat16


def _params(semantics, vmem_bytes):
    return pltpu.CompilerParams(
        dimension_semantics=semantics,
        vmem_limit_bytes=min(int(vmem_bytes), V7X_VMEM_LIMIT_BYTES))


def _nbytes(shape, dtype):
    n = jnp.dtype(dtype).itemsize
    for s in shape:
        n *= s
    return n


def _rmsnorm_kernel(x_ref, g_ref, o_ref):
    x = x_ref[...]
    ms = jnp.mean(x * x, axis=-1, keepdims=True)
    o_ref[...] = ((x * lax.rsqrt(ms + EPS)) * g_ref[...]).astype(o_ref.dtype)


def rmsnorm(x, g, out_dtype, *, rows=256):
    s, d = x.shape
    rows = min(rows, s)
    vmem = 2 * rows * d * (4 + jnp.dtype(out_dtype).itemsize) + 4 * rows * d * 4
    return pl.pallas_call(
        _rmsnorm_kernel,
        grid=(s // rows,),
        in_specs=[pl.BlockSpec((rows, d), lambda i: (i, 0)),
                  pl.BlockSpec((1, d), lambda i: (0, 0))],
        out_specs=pl.BlockSpec((rows, d), lambda i: (i, 0)),
        out_shape=jax.ShapeDtypeStruct((s, d), out_dtype),
        compiler_params=_params(("parallel",), vmem),
        name="rmsnorm",
    )(x, g.reshape(1, d))


def _mm_plain_kernel(x_ref, w_ref, o_ref):
    o_ref[...] = jnp.dot(x_ref[...], w_ref[...],
                         preferred_element_type=F32).astype(o_ref.dtype)


def _mm_bias_res_kernel(x_ref, w_ref, b_ref, r_ref, o_ref):
    y = jnp.dot(x_ref[...], w_ref[...], preferred_element_type=F32)
    o_ref[...] = r_ref[...] + (y + b_ref[...])


def _mm_res_kernel(x_ref, w_ref, r_ref, o_ref):
    y = jnp.dot(x_ref[...], w_ref[...], preferred_element_type=F32)
    o_ref[...] = r_ref[...] + y


def _mm_glu_kernel(x_ref, wa_ref, wg_ref, ba_ref, bg_ref, o_ref):
    x = x_ref[...]
    a = jnp.dot(x, wa_ref[...], preferred_element_type=F32) + ba_ref[...]
    g = jnp.dot(x, wg_ref[...], preferred_element_type=F32) + bg_ref[...]
    o_ref[...] = a * jax.nn.sigmoid(g)


def _wspec(shape, idx, layer):
    if layer is None:
        return pl.BlockSpec(shape, idx)
    return pl.BlockSpec((None,) + shape, lambda *g: (layer,) + idx(*g))


def matmul(x, w, *, out_dtype, layer=None, bias=None, residual=None, tm=1024, tn=512):
    m, k = x.shape
    n = w.shape[-1]
    tm, tn = min(tm, m), min(tn, n)
    in_specs = [pl.BlockSpec((tm, k), lambda i, j: (i, 0)),
                _wspec((k, tn), lambda i, j: (0, j), layer)]
    args = [x, w]
    if residual is None:
        assert bias is None
        body = _mm_plain_kernel
    else:
        if bias is not None:
            in_specs.append(pl.BlockSpec((1, tn), lambda i, j: (0, j)))
            args.append(bias.reshape(1, n))
            body = _mm_bias_res_kernel
        else:
            body = _mm_res_kernel
        in_specs.append(pl.BlockSpec((tm, tn), lambda i, j: (i, j)))
        args.append(residual)
    vmem = (2 * _nbytes((tm, k), BF16) + 2 * _nbytes((k, tn), BF16)
            + 2 * _nbytes((tm, tn), out_dtype) + 6 * _nbytes((tm, tn), F32))
    return pl.pallas_call(
        body,
        grid=(m // tm, n // tn),
        in_specs=in_specs,
        out_specs=pl.BlockSpec((tm, tn), lambda i, j: (i, j)),
        out_shape=jax.ShapeDtypeStruct((m, n), out_dtype),
        compiler_params=_params(("parallel", "parallel"), vmem),
        name="matmul",
    )(*args)


def matmul_glu(x, w, b, *, layer=None, tm=1024, tn=512):
    m, k = x.shape
    n = w.shape[-1] // 2
    tm, tn = min(tm, m), min(tn, n)
    nb = n // tn
    b2 = b.reshape(1, 2 * n)
    vmem = (2 * _nbytes((tm, k), BF16) + 4 * _nbytes((k, tn), BF16)
            + 10 * _nbytes((tm, tn), F32))
    return pl.pallas_call(
        _mm_glu_kernel,
        grid=(m // tm, nb),
        in_specs=[pl.BlockSpec((tm, k), lambda i, j: (i, 0)),
                  _wspec((k, tn), lambda i, j: (0, j), layer),
                  _wspec((k, tn), lambda i, j: (0, j + nb), layer),
                  pl.BlockSpec((1, tn), lambda i, j: (0, j)),
                  pl.BlockSpec((1, tn), lambda i, j: (0, j + nb))],
        out_specs=pl.BlockSpec((tm, tn), lambda i, j: (i, j)),
        out_shape=jax.ShapeDtypeStruct((m, n), F32),
        compiler_params=_params(("parallel", "parallel"), vmem),
        name="matmul_glu",
    )(x, w, w, b2, b2)


CAST_BLOCK_BYTES = 8 * 1024 * 1024


def _cast_kernel(w_ref, o_ref):
    o_ref[...] = w_ref[...].astype(o_ref.dtype)


def cast_bf16(w):
    nl, k, n = w.shape
    rows = k
    while rows * n * 4 > CAST_BLOCK_BYTES and rows % 2 == 0 and (rows // 2) % 16 == 0:
        rows //= 2
    vmem = 2 * _nbytes((rows, n), F32) + 2 * _nbytes((rows, n), BF16) + _nbytes((rows, n), F32)
    return pl.pallas_call(
        _cast_kernel,
        grid=(nl, k // rows),
        in_specs=[pl.BlockSpec((None, rows, n), lambda l, r: (l, r, 0))],
        out_specs=pl.BlockSpec((None, rows, n), lambda l, r: (l, r, 0)),
        out_shape=jax.ShapeDtypeStruct(w.shape, BF16),
        compiler_params=_params(("parallel", "parallel"), vmem),
        name="cast_bf16",
    )(w)


CONV_HALO = 32
CONV_STRIDE = 4
CONV_SUB = 8 * CONV_STRIDE
LANES = 128


def _conv_ln_kernel(u_ref, halo_ref, w_ref, b_ref, g_ref, beta_ref, o_ref, buf_ref, y_ref,
                    *, taps):
    i = pl.program_id(0)
    tt, d = u_ref.shape
    slabs = d // LANES
    first = CONV_HALO - (taps - 1)

    def col_body(c, _):
        col = pl.multiple_of(c * LANES, LANES)
        buf_ref[c, 0:CONV_HALO, :] = jnp.where(i > 0, halo_ref[:, pl.ds(col, LANES)], 0.0)
        buf_ref[c, CONV_HALO:, :] = u_ref[:, pl.ds(col, LANES)]
        w = w_ref[:, pl.ds(col, LANES)]
        bias = jnp.broadcast_to(b_ref[:, pl.ds(col, LANES)], (8, LANES))
        wk = [jnp.broadcast_to(w[k:k + 1, :], (8, LANES)) for k in range(taps)]
        for r0 in range(0, tt, CONV_SUB):
            for j in range(CONV_STRIDE):
                acc = bias
                for k in range(taps):
                    acc = acc + wk[k] * buf_ref[c, pl.ds(first + r0 + j + k, 8, stride=CONV_STRIDE), :]
                y_ref[c, pl.ds(r0 + j, 8, stride=CONV_STRIDE), :] = acc
        return 0

    lax.fori_loop(0, slabs, col_body, 0)

    y = y_ref[...]
    mu = jnp.sum(jnp.sum(y, axis=0), axis=-1, keepdims=True) * (1.0 / d)
    yc = y - mu[None]
    var = jnp.sum(jnp.sum(yc * yc, axis=0), axis=-1, keepdims=True) * (1.0 / d)
    inv = lax.rsqrt(var + EPS)
    for c in range(slabs):
        cols = slice(c * LANES, (c + 1) * LANES)
        z = yc[c] * inv * g_ref[:, cols] + beta_ref[:, cols]
        o_ref[:, cols] = (z * jax.nn.sigmoid(z)).astype(o_ref.dtype)


def conv_ln_swish(u, w, b, ln_g, ln_b, *, rows=128):
    s, d = u.shape
    taps = w.shape[0]
    assert taps - 1 <= CONV_HALO and d % LANES == 0
    rows = min(rows, s)
    assert rows % CONV_HALO == 0 and rows % CONV_SUB == 0
    per = rows // CONV_HALO
    vmem = (2 * _nbytes((rows, d), F32) + 2 * _nbytes((CONV_HALO, d), F32)
            + 2 * _nbytes((rows, d), BF16) + _nbytes((rows + CONV_HALO, d), F32)
            + 6 * _nbytes((rows, d), F32))
    row = lambda v: v.reshape(1, d)
    return pl.pallas_call(
        functools.partial(_conv_ln_kernel, taps=taps),
        grid=(s // rows,),
        in_specs=[pl.BlockSpec((rows, d), lambda i: (i, 0)),
                  pl.BlockSpec((CONV_HALO, d), lambda i: (jnp.maximum(i * per - 1, 0), 0)),
                  pl.BlockSpec((taps, d), lambda i: (0, 0)),
                  pl.BlockSpec((1, d), lambda i: (0, 0)),
                  pl.BlockSpec((1, d), lambda i: (0, 0)),
                  pl.BlockSpec((1, d), lambda i: (0, 0))],
        out_specs=pl.BlockSpec((rows, d), lambda i: (i, 0)),
        out_shape=jax.ShapeDtypeStruct((s, d), BF16),
        scratch_shapes=[pltpu.VMEM((d // LANES, rows + CONV_HALO, LANES), F32),
                        pltpu.VMEM((d // LANES, rows, LANES), F32)],
        compiler_params=_params(("parallel",), vmem),
        name="conv_ln_swish",
    )(u, u, w, row(b), row(ln_g), row(ln_b))


LOG2E = 1.4426950408889634
SIGN_BIT = 0x80000000
MASKED_LOGIT = -1e30


def _neg_abs(x):
    bits = lax.bitcast_convert_type(x, jnp.uint32) | jnp.uint32(SIGN_BIT)
    return lax.bitcast_convert_type(bits, F32)


def _sb_attn_kernel(q_ref, k_ref, v_ref, tri_ref, o_ref, z_buf, c_buf, acc_ref, carry_ref,
                    *, blk, heads_per_step, scale):
    i = pl.program_id(1)
    tri = tri_ref[...]
    nt = (((1,), (1,)), ((), ()))
    hp = heads_per_step
    lanes = [slice(h * HEAD_DIM, (h + 1) * HEAD_DIM) for h in range(hp)]
    qs = [q_ref[:, lanes[h]] for h in range(hp)]

    def key_start(n):
        return pl.multiple_of(jnp.maximum(i - n, 0) * blk, blk)

    def logits_stage(n):
        slot = n % 3
        start = key_start(n)
        for h in range(hp):
            k = k_ref[pl.ds(start, blk), lanes[h]]
            z_buf[slot, h] = (lax.dot_general(qs[h], k, nt, preferred_element_type=F32)
                              * (scale * LOG2E))

    def sums_stage(n, masked):
        if masked:
            t_idx = lax.broadcasted_iota(jnp.int32, (blk, blk), 0)
            s_idx = lax.broadcasted_iota(jnp.int32, (blk, blk), 1)
            mask = s_idx < t_idx
        for h in range(hp):
            z2 = z_buf[n % 3, h]
            sp2 = jnp.maximum(z2, 0.0) + jnp.log2(1.0 + jnp.exp2(_neg_abs(z2)))
            if masked:
                sp2 = jnp.where(mask, sp2, 0.0)
                z_buf[n % 3, h] = jnp.where(mask, z2, MASKED_LOGIT)
            hi = sp2.astype(BF16)
            lo = (sp2 - hi.astype(F32)).astype(BF16)
            c_buf[n % 2, h] = jnp.dot(jnp.concatenate([hi, lo], axis=1), tri,
                                      preferred_element_type=F32)

    def values_stage(n):
        start = key_start(n)
        for h in range(hp):
            c = c_buf[n % 2, h]
            carry = carry_ref[h]
            wide = jnp.concatenate([carry] * (blk // HEAD_DIM), axis=1)
            att = jnp.exp2(z_buf[n % 3, h] + (c + wide))
            v = v_ref[pl.ds(start, blk), lanes[h]]
            acc_ref[h] += jnp.dot(att.astype(BF16), v, preferred_element_type=F32)
            carry_ref[h] = carry + jnp.broadcast_to(c[:, 0:1], (blk, HEAD_DIM))

    acc_ref[...] = jnp.zeros_like(acc_ref)
    carry_ref[...] = jnp.zeros_like(carry_ref)
    logits_stage(0)
    sums_stage(0, True)
    logits_stage(1)

    def body(n, _):
        values_stage(n)
        sums_stage(n + 1, False)
        logits_stage(n + 2)
        return 0

    lax.fori_loop(0, i, body, 0)
    values_stage(i)
    for h in range(hp):
        o_ref[:, lanes[h]] = acc_ref[h].astype(o_ref.dtype)


def stick_breaking_attention(qkv, heads, *, blk=256, heads_per_step=8):
    s = qkv.shape[0]
    blk = min(blk, s)
    hp = heads_per_step
    assert heads % hp == 0 and s % blk == 0 and blk % HEAD_DIM == 0
    groups = heads // hp
    width = hp * HEAD_DIM
    j = lax.broadcasted_iota(jnp.int32, (2 * blk, blk), 0) % blk
    c = lax.broadcasted_iota(jnp.int32, (2 * blk, blk), 1)
    tri = jnp.where(j >= c, -1.0, 0.0).astype(BF16)
    once = pl.Buffered(1)
    vmem = (2 * _nbytes((s, width), BF16) + 4 * _nbytes((blk, width), BF16)
            + 2 * _nbytes((2 * blk, blk), BF16) + 12 * hp * _nbytes((blk, blk), F32))
    return pl.pallas_call(
        functools.partial(_sb_attn_kernel, blk=blk, heads_per_step=hp, scale=HEAD_DIM ** -0.5),
        grid=(groups, s // blk),
        in_specs=[pl.BlockSpec((blk, width), lambda g, i: (i, g)),
                  pl.BlockSpec((s, width), lambda g, i: (0, groups + g), pipeline_mode=once),
                  pl.BlockSpec((s, width), lambda g, i: (0, 2 * groups + g), pipeline_mode=once),
                  pl.BlockSpec((2 * blk, blk), lambda g, i: (0, 0))],
        out_specs=pl.BlockSpec((blk, width), lambda g, i: (i, g)),
        out_shape=jax.ShapeDtypeStruct((s, heads * HEAD_DIM), BF16),
        scratch_shapes=[pltpu.VMEM((3, hp, blk, blk), F32), pltpu.VMEM((2, hp, blk, blk), F32),
                        pltpu.VMEM((hp, blk, HEAD_DIM), F32), pltpu.VMEM((hp, blk, HEAD_DIM), F32)],
        compiler_params=_params(("parallel", "parallel"), vmem),
        name="sb_attention",
    )(qkv, qkv, qkv, tri)


FFN_HALO = 8
FFN_EPI_ROWS = 32
FFN_DOWN_COLS = 256


def _ffn_kernel(x_ref, wg_ref, wu_ref, cw_ref, cb_ref, wd_ref, r_ref, o_ref,
                g0_ref, g1_ref, u0_ref, u1_ref, a0_ref, a1_ref, carry_ref, *, nf):
    i = pl.program_id(0)
    f = pl.program_id(1)
    tm = x_ref.shape[0]
    g_refs, u_refs, a_refs = (g0_ref, g1_ref), (u0_ref, u1_ref), (a0_ref, a1_ref)

    @pl.when(f == 0)
    def _():
        o_ref[...] = r_ref[...]
        for ref in g_refs + u_refs + a_refs:
            ref[...] = jnp.zeros_like(ref)

    def step(p):
        g_new, u_new, a_old = g_refs[p], u_refs[p], a_refs[p]
        g_old, u_old, a_new = g_refs[1 - p], u_refs[1 - p], a_refs[1 - p]

        fe = jnp.clip(f - 1, 0, nf - 1)
        live = jnp.logical_and(f >= 1, f <= nf)
        old_tail = carry_ref[fe]
        g_old[0:FFN_HALO, :] = jnp.where(i > 0, old_tail, 0.0)
        carry_ref[fe] = jnp.where(live, g_old[tm:tm + FFN_HALO, :], old_tail)
        x = x_ref[...]
        g_new[FFN_HALO:, :] = jnp.dot(x, wg_ref[...], preferred_element_type=F32)
        u_new[...] = jnp.dot(x, wu_ref[...], preferred_element_type=F32)

        cw = cw_ref[...]
        cb = cb_ref[...]
        d = o_ref.shape[1]
        pieces = max(1, min(tm // FFN_EPI_ROWS, d // FFN_DOWN_COLS))
        rows, cols = tm // pieces, d // pieces
        a_prev = a_old[...]
        for c in range(pieces):
            o_ref[:, pl.ds(c * cols, cols)] += jnp.dot(
                a_prev, wd_ref[:, pl.ds(c * cols, cols)], preferred_element_type=F32)
            r = c * rows
            conv = (cw[0:1, :] * g_old[pl.ds(FFN_HALO - 2 + r, rows), :]
                    + cw[1:2, :] * g_old[pl.ds(FFN_HALO - 1 + r, rows), :]
                    + cw[2:3, :] * g_old[pl.ds(FFN_HALO + r, rows), :]) + cb
            a_new[pl.ds(r, rows), :] = (
                (conv * jax.nn.sigmoid(conv)) * u_old[pl.ds(r, rows), :]).astype(BF16)

    @pl.when(f % 2 == 0)
    def _():
        step(0)

    @pl.when(f % 2 == 1)
    def _():
        step(1)


def conv_ffn(h, resid, w_gate, w_up, cw, cb, w_down, *, layer=None, tm=512, tf=256):
    s, d = h.shape
    fdim = w_gate.shape[-1]
    tm = min(tm, s)
    assert fdim % tf == 0 and s % tm == 0
    nf = fdim // tf
    once = pl.Buffered(1)

    wspec = lambda shape, idx: _wspec(shape, idx, layer)
    col = lambda i, f: (0, jnp.minimum(f, nf - 1))
    ecol = lambda i, f: (0, jnp.clip(f - 1, 0, nf - 1))
    vmem = (_nbytes((tm, d), BF16) + _nbytes((tm, d), F32) + 2 * _nbytes((tm, d), F32)
            + 6 * _nbytes((d, tf), BF16) + _nbytes((nf, FFN_HALO, tf), F32)
            + 14 * _nbytes((tm, tf), F32) + _nbytes((tm, d), F32))
    return pl.pallas_call(
        functools.partial(_ffn_kernel, nf=nf),
        grid=(s // tm, nf + 2),
        in_specs=[pl.BlockSpec((tm, d), lambda i, f: (i, 0), pipeline_mode=once),
                  wspec((d, tf), col),
                  wspec((d, tf), col),
                  wspec((cw.shape[-2], tf), ecol),
                  wspec((1, tf), ecol),
                  wspec((tf, d), lambda i, f: (jnp.clip(f - 2, 0, nf - 1), 0)),
                  pl.BlockSpec((tm, d), lambda i, f: (i, 0), pipeline_mode=once)],
        out_specs=pl.BlockSpec((tm, d), lambda i, f: (i, 0)),
        out_shape=jax.ShapeDtypeStruct((s, d), F32),
        scratch_shapes=[pltpu.VMEM((tm + FFN_HALO, tf), F32), pltpu.VMEM((tm + FFN_HALO, tf), F32),
                        pltpu.VMEM((tm, tf), F32), pltpu.VMEM((tm, tf), F32),
                        pltpu.VMEM((tm, tf), BF16), pltpu.VMEM((tm, tf), BF16),
                        pltpu.VMEM((nf, FFN_HALO, tf), F32)],
        compiler_params=_params(("arbitrary", "arbitrary"), vmem),
        name="conv_ffn",
    )(h, w_gate, w_up, cw, cb, w_down, resid)


def kernel(x, norm_mix, norm_ffn, final_norm, cv_w_pw1, cv_b_pw1, cv_w_dw, cv_b_dw, cv_ln_g, cv_ln_b, cv_w_pw2, cv_b_pw2, sb_w_qkv, sb_w_o, ff_w_gate, ff_w_up, ff_w_dw, ff_b_dw, ff_w_down):
    bsz, seq, d = x.shape
    depth = norm_mix.shape[0]
    heads = d // HEAD_DIM
    fdim = ff_w_gate.shape[-1]
    w_pw1, w_pw2 = cast_bf16(cv_w_pw1), cast_bf16(cv_w_pw2)
    w_qkv, w_o = cast_bf16(sb_w_qkv), cast_bf16(sb_w_o)
    w_gate, w_up, w_down = cast_bf16(ff_w_gate), cast_bf16(ff_w_up), cast_bf16(ff_w_down)
    ff_b = ff_b_dw.reshape(depth, 1, fdim)
    outs = []
    for b in range(bsz):
        xb = x[b]
        for i in range(depth):
            j = i // 2
            h = rmsnorm(xb, norm_mix[i], BF16)
            if i % 2 == 0:
                u = matmul_glu(h, w_pw1, cv_b_pw1[j], layer=j)
                u = conv_ln_swish(u, cv_w_dw[j], cv_b_dw[j], cv_ln_g[j], cv_ln_b[j])
                xb = matmul(u, w_pw2, layer=j, out_dtype=F32, bias=cv_b_pw2[j], residual=xb)
            else:
                qkv = matmul(h, w_qkv, layer=j, out_dtype=BF16)
                o = stick_breaking_attention(qkv, heads)
                xb = matmul(o, w_o, layer=j, out_dtype=F32, residual=xb)
            h = rmsnorm(xb, norm_ffn[i], BF16)
            xb = conv_ffn(h, xb, w_gate, w_up, ff_w_dw, ff_b, w_down, layer=i)
        outs.append(rmsnorm(xb, final_norm, x.dtype))
    if bsz == 1:
        return outs[0].reshape(1, seq, d)
    return jnp.stack(outs, axis=0)
```
